```python
import math
import jax
import jax.numpy as jnp
from jax import lax
import numpy as np


D_MODEL = 2048
BATCH = 8
SEQ = 2048
DEPTH = 2

HEAD_DIM = 64
N_HEADS_A = 12
DILATED_CONFIGS = ((128, 1), (512, 4), (2048, 16))
DILATED_BLOCK = 64
N_HEADS_B = 10
N_KV_B = 2
WINDOW_B = 128
BLOCK_B = 128
N_HEADS_C = 10
GRID_W = 64
NA_ROWS = 8
NA_COLS = 16
WIDTH_A = N_HEADS_A * HEAD_DIM
WIDTH_B = N_HEADS_B * HEAD_DIM
WIDTH_B_KV = N_KV_B * HEAD_DIM
WIDTH_C = N_HEADS_C * HEAD_DIM
MIX_WIDTH = WIDTH_A + WIDTH_B + WIDTH_C
IN_COLS = 3 * WIDTH_A + WIDTH_B + 2 * WIDTH_B_KV + 3 * WIDTH_C
D_FF = 5632
CONV_WIDTH = 3
ROPE_THETA = 10000.0
EPS = 1e-6
NEG_INF = -1e30

kernel_name = 'hybrid_parallel_local_attention_encoder'


def rms_normalize(x):
    x32 = x.astype(jnp.float32)
    return (x32 * lax.rsqrt(jnp.mean(x32 * x32, axis=-1, keepdims=True) + EPS)).astype(x.dtype)


def rmsnorm(x, gain):
    return rms_normalize(x) * gain


def heads(x, n):
    b, t, _ = x.shape
    return x.reshape(b, t, n, HEAD_DIM).transpose(0, 2, 1, 3)


def merge_heads(x):
    b, h, t, d = x.shape
    return x.transpose(0, 2, 1, 3).reshape(b, t, h * d)


def rotary(x):
    t, dh = x.shape[-2], x.shape[-1]
    inv_freq = ROPE_THETA ** (-jnp.arange(0, dh, 2, dtype=jnp.float32) / dh)
    ang = jnp.arange(t, dtype=jnp.float32)[:, None] * inv_freq[None, :]
    cos = jnp.cos(ang).astype(x.dtype)
    sin = jnp.sin(ang).astype(x.dtype)
    x1, x2 = x[..., : dh // 2], x[..., dh // 2:]
    return jnp.concatenate([x1 * cos - x2 * sin, x2 * cos + x1 * sin], axis=-1)


def banded_attention(q, k, v, half, block):
    n, g, length, dh = q.shape
    nb = length // block
    span = block + 2 * half
    kp = jnp.pad(k, ((0, 0), (half, half), (0, 0)))
    vp = jnp.pad(v, ((0, 0), (half, half), (0, 0)))
    idx = jnp.arange(nb)[:, None] * block + jnp.arange(span)[None, :]
    kb = kp[:, idx]
    vb = vp[:, idx]
    qb = q.reshape(n, g, nb, block, dh)
    s = jnp.einsum('ngbqd,nbkd->ngbqk', qb, kb).astype(jnp.float32) * (dh ** -0.5)
    qpos = jnp.arange(nb)[:, None] * block + jnp.arange(block)[None, :]
    kpos = idx - half
    valid = ((jnp.abs(qpos[:, :, None] - kpos[:, None, :]) <= half)
             & (kpos >= 0)[:, None, :] & (kpos < length)[:, None, :])
    s = jnp.where(valid, s, NEG_INF)
    lse = jax.nn.logsumexp(s, axis=-1)
    p = jnp.exp(s - lse[..., None]).astype(v.dtype)
    o = jnp.einsum('ngbqk,nbkd->ngbqd', p, vb)
    return o.reshape(n, g, length, dh), lse.reshape(n, g, length)


def to_strided(x, r):
    b, h, t, d = x.shape
    return x.reshape(b, h, t // r, r, d).transpose(0, 1, 3, 2, 4).reshape(b * h * r, t // r, d)


def from_strided(x, b, h, r):
    length = x.shape[1]
    rest = x.shape[2:]
    y = x.reshape((b, h, r, length) + rest)
    y = jnp.moveaxis(y, 2, 3)
    return y.reshape((b, h, length * r) + rest)


def dilated_attention(q, k, v):
    b, h, t, dh = q.shape
    outs, lses = [], []
    for window, r in DILATED_CONFIGS:
        length = t // r
        half = window // (2 * r)
        block = math.gcd(length, DILATED_BLOCK)
        o, lse = banded_attention(to_strided(q, r)[:, None], to_strided(k, r), to_strided(v, r), half, block)
        outs.append(from_strided(o[:, 0], b, h, r))
        lses.append(from_strided(lse[:, 0], b, h, r))
    weights = jax.nn.softmax(jnp.stack(lses), axis=0)
    return jnp.einsum('cbht,cbhtd->bhtd', weights.astype(q.dtype), jnp.stack(outs))


def sink_window_attention(q, k, v, sink):
    b, hq, t, dh = q.shape
    g = hq // N_KV_B
    o, lse = banded_attention(q.reshape(b * N_KV_B, g, t, dh), k.reshape(b * N_KV_B, t, dh),
                              v.reshape(b * N_KV_B, t, dh), WINDOW_B, BLOCK_B)
    sink_g = jnp.tile(sink.astype(jnp.float32).reshape(N_KV_B, g), (b, 1))[:, :, None]
    keep = jnp.exp(lse - jnp.logaddexp(lse, sink_g))
    return (o * keep[..., None].astype(o.dtype)).reshape(b, hq, t, dh)


def neighborhood_attention(q, k, v, rpb):
    b, h, t, dh = q.shape
    rows = t // GRID_W
    kr = min(NA_ROWS, rows)
    kc = NA_COLS
    r = jnp.arange(rows)
    c = jnp.arange(GRID_W)
    row_start = jnp.clip(r - kr // 2, 0, rows - kr)
    ridx = row_start[:, None] + jnp.arange(kr)[None, :]
    col_start = jnp.clip(c - kc // 2, 0, GRID_W - kc)
    col_mask = (c[None, :] >= col_start[:, None]) & (c[None, :] < col_start[:, None] + kc)
    qg = q.reshape(b, h, rows, GRID_W, dh)
    kg = k.reshape(b, h, rows, GRID_W, dh)[:, :, ridx]
    vg = v.reshape(b, h, rows, GRID_W, dh)[:, :, ridx]
    s = jnp.einsum('bhrcd,bhrkwd->bhrckw', qg, kg).astype(jnp.float32) * (dh ** -0.5)
    roff = (ridx - r[:, None]) + (NA_ROWS - 1)
    coff = jnp.clip(c[None, :] - c[:, None] + (kc - 1), 0, 2 * kc - 2)
    bias = rpb[:, roff[:, None, :, None], coff[None, :, None, :]]
    s = jnp.where(col_mask[:, None, :], s + bias.astype(jnp.float32)[None], NEG_INF)
    p = jax.nn.softmax(s.reshape(b, h, rows, GRID_W, kr * GRID_W), axis=-1)
    p = p.reshape(s.shape).astype(v.dtype)
    o = jnp.einsum('bhrckw,bhrkwd->bhrcd', p, vg)
    return o.reshape(b, h, t, dh)


def split_projection(proj):
    sizes = (WIDTH_A,) * 3 + (WIDTH_B, WIDTH_B_KV, WIDTH_B_KV) + (WIDTH_C,) * 3
    offsets = np.cumsum(sizes)[:-1].tolist()
    return jnp.split(proj, offsets, axis=-1)


def depthwise_conv(u, w, bias):
    t = u.shape[1]
    pad = CONV_WIDTH // 2
    up = jnp.pad(u, ((0, 0), (pad, pad), (0, 0)))
    return sum(up[:, j:j + t] * w[j] for j in range(CONV_WIDTH)) + bias


def setup_inputs(seed: int = 0) -> dict:
    key = jax.random.key(seed)
    ks = jax.random.split(key, 13)
    nrm = jax.random.normal
    f32 = jnp.float32
    return {
        'x': nrm(ks[0], (BATCH, SEQ, D_MODEL), f32),
        'ln_attn': 1.0 + 0.02 * nrm(ks[1], (DEPTH, D_MODEL), f32),
        'w_in': nrm(ks[2], (DEPTH, D_MODEL, IN_COLS), f32) * D_MODEL ** -0.5,
        'sink_b': 0.5 * nrm(ks[3], (DEPTH, N_HEADS_B), f32),
        'rpb_c': 0.1 * nrm(ks[4], (DEPTH, N_HEADS_C, 2 * NA_ROWS - 1, 2 * NA_COLS - 1), f32),
        'mix_gain': 1.0 + 0.02 * nrm(ks[5], (DEPTH, MIX_WIDTH), f32),
        'w_out': nrm(ks[6], (DEPTH, MIX_WIDTH, D_MODEL), f32) * MIX_WIDTH ** -0.5,
        'ln_ffn': 1.0 + 0.02 * nrm(ks[7], (DEPTH, D_MODEL), f32),
        'w_up': nrm(ks[8], (DEPTH, D_MODEL, 2 * D_FF), f32) * D_MODEL ** -0.5,
        'conv_w': nrm(ks[9], (DEPTH, CONV_WIDTH, 2 * D_FF), f32) * CONV_WIDTH ** -0.5,
        'conv_b': 0.01 * nrm(ks[10], (DEPTH, 2 * D_FF), f32),
        'w_down': nrm(ks[11], (DEPTH, D_FF, D_MODEL), f32) * D_FF ** -0.5,
        'ln_final': 1.0 + 0.02 * nrm(ks[12], (D_MODEL,), f32),
    }


def reference(x, ln_attn, w_in, sink_b, rpb_c, mix_gain, w_out, ln_ffn, w_up, conv_w, conv_b, w_down, ln_final):
    for l in range(DEPTH):
        h = rmsnorm(x, ln_attn[l])
        qa, ka, va, qb, kb, vb, qc, kc, vc = split_projection(h @ w_in[l])
        oa = dilated_attention(rotary(heads(qa, N_HEADS_A)), rotary(heads(ka, N_HEADS_A)), heads(va, N_HEADS_A))
        ob = sink_window_attention(rotary(heads(qb, N_HEADS_B)), rotary(heads(kb, N_KV_B)), heads(vb, N_KV_B), sink_b[l])
        oc = neighborhood_attention(heads(qc, N_HEADS_C), heads(kc, N_HEADS_C), heads(vc, N_HEADS_C), rpb_c[l])
        mixed = jnp.concatenate([rms_normalize(merge_heads(oa)), rms_normalize(merge_heads(ob)),
                                 rms_normalize(merge_heads(oc))], axis=-1) * mix_gain[l]
        x = x + mixed @ w_out[l]
        h = rmsnorm(x, ln_ffn[l])
        u = depthwise_conv(h @ w_up[l], conv_w[l], conv_b[l])
        gate, val = jnp.split(u, 2, axis=-1)
        x = x + (jax.nn.silu(gate) * val) @ w_down[l]
    return rmsnorm(x, ln_final)
```

```python
import functools
import math

import jax
import jax.numpy as jnp
import numpy as np
from jax import lax
from jax.experimental import pallas as pl
from jax.experimental.pallas import tpu as pltpu

F32 = jnp.float32
BF16 = jnp.bfloat16

LANES = 128
HEAD_DIM = 64
N_HEADS_A, N_HEADS_B, N_KV_B, N_HEADS_C = 12, 10, 2, 10
PAIRS_A, PAIRS_B, PAIRS_C = N_HEADS_A // 2, N_HEADS_B // 2, N_HEADS_C // 2
DILATIONS = (1, 4, 16)
HALF_A = 64
WINDOW_B = 128
GRID_W, NA_ROWS, NA_COLS = 64, 8, 16
ROPE_THETA = 10000.0
EPS = 1e-6
NEG_INF = -1e30
Q_SCALE = HEAD_DIM ** -0.5

G_QA, G_KA, G_VA = 0, 6, 12
G_QB, G_KB, G_VB = 18, 23, 24
G_QC, G_KC, G_VC = 25, 30, 35
N_GROUPS = 40

VMEM_LIMIT = 56 * 1024 * 1024


def _cparams(*sem):
    return pltpu.CompilerParams(dimension_semantics=sem, vmem_limit_bytes=VMEM_LIMIT)


def _rmsnorm_kernel(x_ref, g_ref, o_ref):
    x = x_ref[...]
    ms = jnp.mean(x * x, axis=-1, keepdims=True)
    o_ref[...] = (x * lax.rsqrt(ms + EPS) * g_ref[...]).astype(o_ref.dtype)


def _rmsnorm(x2d, gain, out_dtype, tm=512):
    m, d = x2d.shape
    return pl.pallas_call(
        _rmsnorm_kernel,
        grid=(m // tm,),
        in_specs=[pl.BlockSpec((tm, d), lambda i: (i, 0)), pl.BlockSpec((1, d), lambda i: (0, 0))],
        out_specs=pl.BlockSpec((tm, d), lambda i: (i, 0)),
        out_shape=jax.ShapeDtypeStruct((m, d), out_dtype),
        compiler_params=_cparams("parallel"),
    )(x2d, gain.reshape(1, d))


def _inproj_kernel(h_ref, w_ref, cos_ref, sin_ref, o_ref, *, groups_per_tile):
    j = pl.program_id(1)
    acc = jnp.dot(h_ref[...], w_ref[...], preferred_element_type=F32)
    lane = lax.broadcasted_iota(jnp.int32, (acc.shape[0], LANES), 1)
    first_half = (lane % HEAD_DIM) < (HEAD_DIM // 2)
    for gi in range(groups_per_tile):
        g = j * groups_per_tile + gi
        is_rot = (g < G_VA) | ((g >= G_QB) & (g < G_VB))
        is_q = (g < G_KA) | ((g >= G_QB) & (g < G_KB)) | ((g >= G_QC) & (g < G_KC))
        qs = jnp.where(is_q, Q_SCALE, 1.0).astype(F32)
        xg = acc[:, gi * LANES:(gi + 1) * LANES]

        @pl.when(is_rot)
        def _():
            rot = jnp.where(first_half, pltpu.roll(xg, LANES - 32, 1), pltpu.roll(xg, 32, 1))
            o_ref[0, gi] = ((xg * cos_ref[...] + rot * sin_ref[...]) * qs).astype(o_ref.dtype)

        @pl.when(jnp.logical_not(is_rot))
        def _():
            o_ref[0, gi] = (xg * qs).astype(o_ref.dtype)


def _inproj(h, w, cos, sin, batch, seq, tm=1024, tn=1024):
    m, d = h.shape
    n = w.shape[1]
    gpt = tn // LANES
    tpb = seq // tm
    return pl.pallas_call(
        functools.partial(_inproj_kernel, groups_per_tile=gpt),
        grid=(m // tm, n // tn),
        in_specs=[
            pl.BlockSpec((tm, d), lambda i, j: (i, 0)),
            pl.BlockSpec((d, tn), lambda i, j: (0, j)),
            pl.BlockSpec((tm, LANES), lambda i, j: (i % tpb, 0)),
            pl.BlockSpec((tm, LANES), lambda i, j: (i % tpb, 0)),
        ],
        out_specs=pl.BlockSpec((1, gpt, tm, LANES), lambda i, j: (i // tpb, j, i % tpb, 0)),
        out_shape=jax.ShapeDtypeStruct((batch, n // LANES, seq, LANES), BF16),
        compiler_params=_cparams("parallel", "arbitrary"),
    )(h, w, cos, sin)


def _pair_blocks(win):
    lane = lax.broadcasted_iota(jnp.int32, win.shape, 1)
    lo = lane < HEAD_DIM
    zero = jnp.zeros_like(win)
    return jnp.concatenate([jnp.where(lo, win, zero), jnp.where(lo, zero, win)], axis=0)


def _pair_attention(q, kblk, vblk, bias0, bias1):
    w = kblk.shape[0] // 2
    s = lax.dot_general(q, kblk, (((1,), (1,)), ((), ())), preferred_element_type=F32)
    s0 = s[:, :w] + bias0
    s1 = s[:, w:] + bias1
    m0 = jnp.max(s0, axis=-1, keepdims=True)
    m1 = jnp.max(s1, axis=-1, keepdims=True)
    p0 = jnp.exp(s0 - m0)
    p1 = jnp.exp(s1 - m1)
    l0 = jnp.sum(p0, axis=-1, keepdims=True)
    l1 = jnp.sum(p1, axis=-1, keepdims=True)
    p = jnp.concatenate([p0, p1], axis=-1).astype(BF16)
    acc = jnp.dot(p, vblk, preferred_element_type=F32)
    lo = lax.broadcasted_iota(jnp.int32, acc.shape, 1) < HEAD_DIM
    return acc, jnp.where(lo, m0, m1), jnp.where(lo, l0, l1)


def _band_bias(tq, w, offset, half):
    rel = (lax.broadcasted_iota(jnp.int32, (tq, w), 1) - lax.broadcasted_iota(jnp.int32, (tq, w), 0)) + offset
    return jnp.where((rel <= half) & (rel >= -half), 0.0, NEG_INF).astype(F32)


A_BLOCK = 128
A_WIN = A_BLOCK + 2 * HALF_A


def _attn_a_kernel(q1_ref, k1_ref, v1_ref, q4_ref, k4_ref, v4_ref, q16_ref, k16_ref, v16_ref,
                   o_ref, o1_scr, l1_scr, o4_scr, l4_scr):
    def banded(q_ref, k_ref, v_ref, res, length, o_scr, l_scr):
        col, base = res * LANES, res * length

        def body(blk, carry):
            q0 = pl.multiple_of(blk * A_BLOCK, A_BLOCK)
            ws = pl.multiple_of(jnp.clip(q0 - HALF_A, 0, length - A_WIN), HALF_A)
            q = q_ref[0, 0, pl.ds(q0, A_BLOCK), col:col + LANES]
            kw = k_ref[0, 0, pl.ds(ws, A_WIN), col:col + LANES]
            vw = v_ref[0, 0, pl.ds(ws, A_WIN), col:col + LANES]
            bias = _band_bias(A_BLOCK, A_WIN, ws - q0, HALF_A)
            acc, m, l = _pair_attention(q, _pair_blocks(kw), _pair_blocks(vw), bias, bias)
            o_scr[pl.ds(base + q0, A_BLOCK), :] = acc / l
            l_scr[pl.ds(base + q0, A_BLOCK), :] = m + jnp.log(l)
            return carry
        lax.fori_loop(0, length // A_BLOCK, body, 0)

    seq = q1_ref.shape[2]
    len4 = seq // 4
    banded(q1_ref, k1_ref, v1_ref, 0, seq, o1_scr, l1_scr)
    for res in range(4):
        banded(q4_ref, k4_ref, v4_ref, res, len4, o4_scr, l4_scr)

    len16 = seq // 16
    bias16 = _band_bias(len16, len16, 0, HALF_A)
    for res in range(16):
        col = res * LANES
        q = q16_ref[0, 0, :, col:col + LANES]
        acc, m, l = _pair_attention(q, _pair_blocks(k16_ref[0, 0, :, col:col + LANES]),
                                    _pair_blocks(v16_ref[0, 0, :, col:col + LANES]), bias16, bias16)
        o16 = acc / l
        lse16 = m + jnp.log(l)
        o1 = o1_scr[pl.ds(res, len16, stride=16), :]
        lse1 = l1_scr[pl.ds(res, len16, stride=16), :]
        r4 = (res % 4) * len4 + res // 4
        o4 = o4_scr[pl.ds(r4, len16, stride=4), :]
        lse4 = l4_scr[pl.ds(r4, len16, stride=4), :]
        mx = jnp.maximum(jnp.maximum(lse1, lse4), lse16)
        e1 = jnp.exp(lse1 - mx)
        e4 = jnp.exp(lse4 - mx)
        e16 = jnp.exp(lse16 - mx)
        o_ref[0, 0, :, col:col + LANES] = ((e1 * o1 + e4 * o4 + e16 * o16) / (e1 + e4 + e16)).astype(o_ref.dtype)


def _attn_a(proj, batch, seq):
    views = [proj.reshape(batch, N_GROUPS, seq // r, r * LANES) for r in DILATIONS]
    in_specs, args = [], []
    for view, r in zip(views, DILATIONS):
        for goff in (G_QA, G_KA, G_VA):
            in_specs.append(pl.BlockSpec((1, 1, seq // r, r * LANES),
                                         lambda b, g, goff=goff: (b, goff + g, 0, 0)))
            args.append(view)
    out = pl.pallas_call(
        _attn_a_kernel,
        grid=(batch, PAIRS_A),
        in_specs=in_specs,
        out_specs=pl.BlockSpec((1, 1, seq // 16, 16 * LANES), lambda b, g: (b, g, 0, 0)),
        out_shape=jax.ShapeDtypeStruct((batch, PAIRS_A, seq // 16, 16 * LANES), BF16),
        scratch_shapes=[pltpu.VMEM((seq, LANES), F32)] * 4,
        compiler_params=_cparams("parallel", "parallel"),
    )(*args)
    return out.reshape(batch, PAIRS_A, seq, LANES)


B_BLOCK = 128
B_WIN = B_BLOCK + 2 * WINDOW_B


def _attn_b_kernel(q_ref, k_ref, v_ref, sink_ref, o_ref):
    p = pl.program_id(1)
    seq = q_ref.shape[2]
    heads_per_kv = N_HEADS_B // N_KV_B
    lane = lax.broadcasted_iota(jnp.int32, (B_WIN, LANES), 1)
    lo = lane < HEAD_DIM
    swap0 = (2 * p) // heads_per_kv != 0
    swap1 = (2 * p + 1) // heads_per_kv != 1

    def place(win):
        x = win.astype(F32)
        xs = pltpu.roll(x, HEAD_DIM, 1)
        top = jnp.where(lo, jnp.where(swap0, xs, x), 0.0)
        bot = jnp.where(lo, 0.0, jnp.where(swap1, xs, x))
        return jnp.concatenate([top, bot], axis=0).astype(BF16)

    def body(blk, carry):
        q0 = pl.multiple_of(blk * B_BLOCK, B_BLOCK)
        ws = pl.multiple_of(jnp.clip(q0 - WINDOW_B, 0, seq - B_WIN), B_BLOCK)
        q = q_ref[0, 0, pl.ds(q0, B_BLOCK), :]
        kblk = place(k_ref[0, 0, pl.ds(ws, B_WIN), :])
        vblk = place(v_ref[0, 0, pl.ds(ws, B_WIN), :])
        bias = _band_bias(B_BLOCK, B_WIN, ws - q0, WINDOW_B)
        acc, m, l = _pair_attention(q, kblk, vblk, bias, bias)
        o_ref[0, 0, pl.ds(q0, B_BLOCK), :] = (acc / (l + jnp.exp(sink_ref[0] - m))).astype(o_ref.dtype)
        return carry

    lax.fori_loop(0, seq // B_BLOCK, body, 0)


def _attn_b(proj, sink_lanes, batch, seq):
    blk = (1, 1, seq, LANES)
    return pl.pallas_call(
        _attn_b_kernel,
        grid=(batch, PAIRS_B),
        in_specs=[pl.BlockSpec(blk, lambda b, p: (b, G_QB + p, 0, 0)),
                  pl.BlockSpec(blk, lambda b, p: (b, G_KB, 0, 0)),
                  pl.BlockSpec(blk, lambda b, p: (b, G_VB, 0, 0)),
                  pl.BlockSpec((1, 1, LANES), lambda b, p: (p, 0, 0))],
        out_specs=pl.BlockSpec(blk, lambda b, p: (b, p, 0, 0)),
        out_shape=jax.ShapeDtypeStruct((batch, PAIRS_B, seq, LANES), BF16),
        compiler_params=_cparams("parallel", "parallel"),
    )(proj, proj, proj, sink_lanes)


def _attn_c_kernel(q_ref, k_ref, v_ref, tab_ref, o_ref):
    seq = q_ref.shape[2]
    rows = seq // GRID_W
    kr = min(NA_ROWS, rows)
    win = kr * GRID_W

    def body(r, carry):
        rs = jnp.clip(r - kr // 2, 0, rows - kr)
        q0 = pl.multiple_of(r * GRID_W, GRID_W)
        ws = pl.multiple_of(rs * GRID_W, GRID_W)
        q = q_ref[0, 0, pl.ds(q0, GRID_W), :]
        kblk = _pair_blocks(k_ref[0, 0, pl.ds(ws, win), :])
        vblk = _pair_blocks(v_ref[0, 0, pl.ds(ws, win), :])
        tab = tab_ref[0, r - rs]
        acc, m, l = _pair_attention(q, kblk, vblk, tab[:, :win], tab[:, win:])
        o_ref[0, 0, pl.ds(q0, GRID_W), :] = (acc / l).astype(o_ref.dtype)
        return carry

    lax.fori_loop(0, rows, body, 0)


def _attn_c(proj, table, batch, seq):
    blk = (1, 1, seq, LANES)
    return pl.pallas_call(
        _attn_c_kernel,
        grid=(PAIRS_C, batch),
        in_specs=[pl.BlockSpec(blk, lambda p, b: (b, G_QC + p, 0, 0)),
                  pl.BlockSpec(blk, lambda p, b: (b, G_KC + p, 0, 0)),
                  pl.BlockSpec(blk, lambda p, b: (b, G_VC + p, 0, 0)),
                  pl.BlockSpec((1,) + table.shape[1:], lambda p, b: (p, 0, 0, 0))],
        out_specs=pl.BlockSpec(blk, lambda p, b: (b, p, 0, 0)),
        out_shape=jax.ShapeDtypeStruct((batch, PAIRS_C, seq, LANES), BF16),
        compiler_params=_cparams("parallel", "parallel"),
    )(proj, proj, proj, table)


def _neighbourhood_table(rpb, seq):
    rows = seq // GRID_W
    kr = min(NA_ROWS, rows)
    d = np.arange(kr)
    k = np.arange(kr)
    c = np.arange(GRID_W)
    roff = k[None, :] + (NA_ROWS - 1) - d[:, None]
    coff = np.clip(c[None, :] - c[:, None] + (NA_COLS - 1), 0, 2 * NA_COLS - 2)
    col_start = np.clip(c - NA_COLS // 2, 0, GRID_W - NA_COLS)
    col_mask = (c[None, :] >= col_start[:, None]) & (c[None, :] < col_start[:, None] + NA_COLS)
    bias = rpb.astype(F32)[:, roff[:, None, :, None], coff[None, :, None, :]]
    bias = jnp.where(col_mask[None, None, :, None, :], bias, NEG_INF)
    h = rpb.shape[0]
    bias = bias.reshape(h // 2, 2, kr, GRID_W, kr * GRID_W)
    return bias.transpose(0, 2, 3, 1, 4).reshape(h // 2, kr, GRID_W, 2 * kr * GRID_W)


def _outproj_kernel(oa_ref, ob_ref, oc_ref, gain_ref, w_ref, x_ref, ln_ref, xo_ref, ho_ref, mixed_scr):
    col = 0
    for ref in (oa_ref, ob_ref, oc_ref):
        n = ref.shape[1]
        o = ref[0].astype(F32)
        ss = jnp.sum(jnp.sum(o * o, axis=0), axis=-1, keepdims=True)
        inv = lax.rsqrt(ss / (n * LANES) + EPS)
        for g in range(n):
            sl = slice(col * LANES, (col + 1) * LANES)
            mixed_scr[:, sl] = (o[g] * inv * gain_ref[:, sl]).astype(BF16)
            col += 1
    xn = x_ref[...] + jnp.dot(mixed_scr[...], w_ref[...], preferred_element_type=F32)
    xo_ref[...] = xn
    ms = jnp.mean(xn * xn, axis=-1, keepdims=True)
    ho_ref[...] = (xn * lax.rsqrt(ms + EPS) * ln_ref[...]).astype(ho_ref.dtype)


def _outproj(oa, ob, oc, gain, w, x2d, ln, batch, seq, tm=256):
    m, d = x2d.shape
    tpb = seq // tm

    def ospec(o):
        return pl.BlockSpec((1, o.shape[1], tm, LANES), lambda i: (i // tpb, 0, i % tpb, 0))

    row = pl.BlockSpec((tm, d), lambda i: (i, 0))
    vec = pl.BlockSpec((1, d), lambda i: (0, 0))
    return pl.pallas_call(
        _outproj_kernel,
        grid=(m // tm,),
        in_specs=[ospec(oa), ospec(ob), ospec(oc), vec, pl.BlockSpec(w.shape, lambda i: (0, 0)), row, vec],
        out_specs=[row, row],
        out_shape=[jax.ShapeDtypeStruct((m, d), F32), jax.ShapeDtypeStruct((m, d), BF16)],
        scratch_shapes=[pltpu.VMEM((tm, d), BF16)],
        compiler_params=_cparams("parallel"),
    )(oa, ob, oc, gain.reshape(1, d), w, x2d, ln.reshape(1, d))


def _ffn_up_kernel(h_ref, wg_ref, wv_ref, cwg_ref, cwv_ref, cbg_ref, cbv_ref, o_ref):
    h = h_ref[...]
    seq = h.shape[0]
    row = lax.broadcasted_iota(jnp.int32, (seq, wg_ref.shape[1]), 0)
    first, last = row == 0, row == seq - 1

    def conv(u, cw_ref, cb_ref):
        prev = jnp.where(first, 0.0, pltpu.roll(u, 1, 0))
        nxt = jnp.where(last, 0.0, pltpu.roll(u, seq - 1, 0))
        return prev * cw_ref[0:1, :] + u * cw_ref[1:2, :] + nxt * cw_ref[2:3, :] + cb_ref[...]

    g = conv(jnp.dot(h, wg_ref[...], preferred_element_type=F32), cwg_ref, cbg_ref)
    v = conv(jnp.dot(h, wv_ref[...], preferred_element_type=F32), cwv_ref, cbv_ref)
    o_ref[...] = (g / (1.0 + jnp.exp(-g)) * v).astype(o_ref.dtype)


def _ffn_up(h, w_up, conv_w, conv_b, batch, seq, tf=256):
    m, d = h.shape
    d_ff = w_up.shape[1] // 2
    nf = d_ff // tf
    cw = conv_w.shape[0]
    conv_b = conv_b.reshape(1, 2 * d_ff)
    return pl.pallas_call(
        _ffn_up_kernel,
        grid=(batch, nf),
        in_specs=[pl.BlockSpec((seq, d), lambda b, j: (b, 0)),
                  pl.BlockSpec((d, tf), lambda b, j: (0, j)),
                  pl.BlockSpec((d, tf), lambda b, j: (0, j + nf)),
                  pl.BlockSpec((cw, tf), lambda b, j: (0, j)),
                  pl.BlockSpec((cw, tf), lambda b, j: (0, j + nf)),
                  pl.BlockSpec((1, tf), lambda b, j: (0, j)),
                  pl.BlockSpec((1, tf), lambda b, j: (0, j + nf))],
        out_specs=pl.BlockSpec((seq, tf), lambda b, j: (b, j)),
        out_shape=jax.ShapeDtypeStruct((m, d_ff), BF16),
        compiler_params=_cparams("parallel", "arbitrary"),
    )(h, w_up, w_up, conv_w, conv_w, conv_b, conv_b)


def _ffn_down_kernel(a_ref, w_ref, x_ref, ln_ref, *refs, final):
    acc_scr = refs[-1]
    k = pl.program_id(1)

    @pl.when(k == 0)
    def _():
        acc_scr[...] = jnp.zeros_like(acc_scr)

    acc_scr[...] += jnp.dot(a_ref[...], w_ref[...], preferred_element_type=F32)

    @pl.when(k == pl.num_programs(1) - 1)
    def _():
        xn = x_ref[...] + acc_scr[...]
        ms = jnp.mean(xn * xn, axis=-1, keepdims=True)
        hn = xn * lax.rsqrt(ms + EPS) * ln_ref[...]
        if final:
            refs[0][...] = hn
        else:
            refs[0][...] = xn
            refs[1][...] = hn.astype(refs[1].dtype)


def _ffn_down(act, w, x2d, ln, final, tm=512, tk=1408):
    m, d = x2d.shape
    kdim = act.shape[1]
    row = pl.BlockSpec((tm, d), lambda i, k: (i, 0))
    if final:
        out_specs, out_shape = row, jax.ShapeDtypeStruct((m, d), F32)
    else:
        out_specs = [row, row]
        out_shape = [jax.ShapeDtypeStruct((m, d), F32), jax.ShapeDtypeStruct((m, d), BF16)]
    return pl.pallas_call(
        functools.partial(_ffn_down_kernel, final=final),
        grid=(m // tm, kdim // tk),
        in_specs=[pl.BlockSpec((tm, tk), lambda i, k: (i, k)),
                  pl.BlockSpec((tk, d), lambda i, k: (k, 0)),
                  row, pl.BlockSpec((1, d), lambda i, k: (0, 0))],
        out_specs=out_specs,
        out_shape=out_shape,
        scratch_shapes=[pltpu.VMEM((tm, d), F32)],
        compiler_params=_cparams("parallel", "arbitrary"),
    )(act, w, x2d, ln.reshape(1, d))


def _rotary_tables(seq):
    inv_freq = ROPE_THETA ** (-jnp.arange(0, HEAD_DIM, 2, dtype=F32) / HEAD_DIM)
    ang = jnp.arange(seq, dtype=F32)[:, None] * inv_freq[None, :]
    cos, sin = jnp.cos(ang), jnp.sin(ang)
    cos = jnp.tile(cos, (1, 2 * LANES // HEAD_DIM))
    sin = jnp.tile(jnp.concatenate([-sin, sin], axis=-1), (1, LANES // HEAD_DIM))
    return cos, sin


def kernel(x, ln_attn, w_in, sink_b, rpb_c, mix_gain, w_out, ln_ffn, w_up, conv_w, conv_b, w_down, ln_final):
    batch, seq, d = x.shape
    depth = w_in.shape[0]
    m = batch * seq
    x2d = x.reshape(m, d)
    cos, sin = _rotary_tables(seq)
    h = _rmsnorm(x2d, ln_attn[0], BF16)
    for l in range(depth):
        proj = _inproj(h, w_in[l].astype(BF16), cos, sin, batch, seq)
        sink_lanes = jnp.repeat(sink_b[l].astype(F32), HEAD_DIM).reshape(PAIRS_B, 1, LANES)
        oa = _attn_a(proj, batch, seq)
        ob = _attn_b(proj, sink_lanes, batch, seq)
        oc = _attn_c(proj, _neighbourhood_table(rpb_c[l], seq), batch, seq)
        x2d, h = _outproj(oa, ob, oc, mix_gain[l], w_out[l].astype(BF16), x2d, ln_ffn[l], batch, seq)
        act = _ffn_up(h, w_up[l].astype(BF16), conv_w[l], conv_b[l], batch, seq)
        if l + 1 < depth:
            x2d, h = _ffn_down(act, w_down[l].astype(BF16), x2d, ln_attn[l + 1], final=False)
        else:
            out = _ffn_down(act, w_down[l].astype(BF16), x2d, ln_final, final=True)
    return out.reshape(batch, seq, d)
```

```python
import functools

import jax
import jax.numpy as jnp
import numpy as np
from jax import lax
from jax.experimental import pallas as pl
from jax.experimental.pallas import tpu as pltpu

F32 = jnp.float32
BF16 = jnp.bfloat16

LANES = 128
MXU_N = 256
HEAD_DIM = 64
N_HEADS_A, N_HEADS_B, N_KV_B, N_HEADS_C = 12, 10, 2, 10
PAIRS_A, PAIRS_B, PAIRS_C = N_HEADS_A // 2, N_HEADS_B // 2, N_HEADS_C // 2
HALF_A = 64
WINDOW_B = 128
GRID_W, NA_ROWS, NA_COLS = 64, 8, 16
ROPE_THETA = 10000.0
EPS = 1e-6
NEG_INF = -1e30
Q_SCALE = HEAD_DIM ** -0.5

GROUPS_A = 3 * PAIRS_A
G_QB, G_KB, G_VB = 0, PAIRS_B, PAIRS_B + 1
G_QC, G_KC, G_VC = PAIRS_B + 2, PAIRS_B + 2 + PAIRS_C, PAIRS_B + 2 + 2 * PAIRS_C
GROUPS_BC = G_VC + PAIRS_C

VMEM_LIMIT = 56 * 1024 * 1024


def _cparams(*sem):
    return pltpu.CompilerParams(dimension_semantics=sem, vmem_limit_bytes=VMEM_LIMIT)


def _resident(shape, index_map):
    return pl.BlockSpec(shape, index_map, pipeline_mode=pl.Buffered(1))


def _rmsnorm_kernel(x_ref, g_ref, o_ref):
    x = x_ref[...]
    ms = jnp.mean(x * x, axis=-1, keepdims=True)
    o_ref[...] = (x * lax.rsqrt(ms + EPS) * g_ref[...]).astype(o_ref.dtype)


def _rmsnorm(x2d, gain, out_dtype, tm=512):
    m, d = x2d.shape
    return pl.pallas_call(
        _rmsnorm_kernel,
        grid=(m // tm,),
        in_specs=[pl.BlockSpec((tm, d), lambda i: (i, 0)), pl.BlockSpec((1, d), lambda i: (0, 0))],
        out_specs=pl.BlockSpec((tm, d), lambda i: (i, 0)),
        out_shape=jax.ShapeDtypeStruct((m, d), out_dtype),
        compiler_params=_cparams("parallel"),
    )(x2d, gain.reshape(1, d))


N_STAGE = 4


def _inproj_kernel(h_ref, w_ref, cos_ref, sin_ref, nat_ref, s4_ref, s16_ref, bc_ref, stage_scr):
    tm = h_ref.shape[0]
    h = h_ref[...]
    lane = lax.broadcasted_iota(jnp.int32, (tm, LANES), 1)
    first_half = (lane % HEAD_DIM) < (HEAD_DIM // 2)
    cos, sin = cos_ref[...], sin_ref[...]

    def rotary(xg, scale):
        rot = jnp.where(first_half, pltpu.roll(xg, LANES - 32, 1), pltpu.roll(xg, 32, 1))
        y = xg * cos + rot * sin
        return y * scale if scale != 1.0 else y

    n_groups = GROUPS_A + GROUPS_BC
    for c in range(n_groups * LANES // MXU_N):
        acc = jnp.dot(h, w_ref[:, c * MXU_N:(c + 1) * MXU_N], preferred_element_type=F32)
        for gi in range(MXU_N // LANES):
            g = c * (MXU_N // LANES) + gi
            xg = acc[:, gi * LANES:(gi + 1) * LANES]
            if g < GROUPS_A:
                if g < PAIRS_A:
                    y = rotary(xg, Q_SCALE)
                elif g < 2 * PAIRS_A:
                    y = rotary(xg, 1.0)
                else:
                    y = xg
                nat_ref[0, g] = y.astype(nat_ref.dtype)
                base = (g % N_STAGE) * tm
                stage_scr[pl.ds(base, tm), :] = y
                for r in range(4):
                    s4_ref[0, g, r] = stage_scr[pl.ds(base + r, tm // 4, stride=4), :].astype(s4_ref.dtype)
                for r in range(16):
                    s16_ref[0, g, r] = stage_scr[pl.ds(base + r, tm // 16, stride=16), :].astype(s16_ref.dtype)
            else:
                gb = g - GROUPS_A
                if gb < G_KB:
                    y = rotary(xg, Q_SCALE)
                elif gb < G_VB:
                    y = rotary(xg, 1.0)
                elif G_QC <= gb < G_KC:
                    y = xg * Q_SCALE
                else:
                    y = xg
                bc_ref[0, gb] = y.astype(bc_ref.dtype)


def _inproj(h, w, cos, sin, batch, seq, tm=512):
    m, d = h.shape
    n = w.shape[1]
    tpb = seq // tm
    return pl.pallas_call(
        _inproj_kernel,
        grid=(m // tm,),
        in_specs=[
            pl.BlockSpec((tm, d), lambda i: (i, 0)),
            _resident((d, n), lambda i: (0, 0)),
            pl.BlockSpec((tm, LANES), lambda i: (i % tpb, 0)),
            pl.BlockSpec((tm, LANES), lambda i: (i % tpb, 0)),
        ],
        out_specs=[
            pl.BlockSpec((1, GROUPS_A, tm, LANES), lambda i: (i // tpb, 0, i % tpb, 0)),
            pl.BlockSpec((1, GROUPS_A, 4, tm // 4, LANES), lambda i: (i // tpb, 0, 0, i % tpb, 0)),
            pl.BlockSpec((1, GROUPS_A, 16, tm // 16, LANES), lambda i: (i // tpb, 0, 0, i % tpb, 0)),
            pl.BlockSpec((1, GROUPS_BC, tm, LANES), lambda i: (i // tpb, 0, i % tpb, 0)),
        ],
        out_shape=[
            jax.ShapeDtypeStruct((batch, GROUPS_A, seq, LANES), BF16),
            jax.ShapeDtypeStruct((batch, GROUPS_A, 4, seq // 4, LANES), BF16),
            jax.ShapeDtypeStruct((batch, GROUPS_A, 16, seq // 16, LANES), BF16),
            jax.ShapeDtypeStruct((batch, GROUPS_BC, seq, LANES), BF16),
        ],
        scratch_shapes=[pltpu.VMEM((N_STAGE * tm, LANES), F32)],
        compiler_params=_cparams("parallel"),
    )(h, w, cos, sin)


def _pair_attention(q, k, v, bias0, bias1):
    tq = q.shape[0]
    lo = lax.broadcasted_iota(jnp.int32, q.shape, 1) < HEAD_DIM
    zero = jnp.zeros_like(q)
    q2 = jnp.concatenate([jnp.where(lo, q, zero), jnp.where(lo, zero, q)], axis=0)
    s = lax.dot_general(q2, k, (((1,), (1,)), ((), ())), preferred_element_type=F32)
    s = jnp.concatenate([s[:tq] + bias0, s[tq:] + bias1], axis=0)
    m = jnp.max(s, axis=-1, keepdims=True)
    p = jnp.exp(s - m).astype(BF16)
    r = jnp.dot(p, jnp.concatenate([v, jnp.ones_like(v)], axis=1), preferred_element_type=F32)
    acc = jnp.where(lo, r[:tq, :LANES], r[tq:, :LANES])
    l = jnp.where(lo, r[:tq, LANES:], r[tq:, LANES:])
    return acc, jnp.where(lo, m[:tq], m[tq:]), l


def _band_bias(tq, w, offset, half):
    rel = (lax.broadcasted_iota(jnp.int32, (tq, w), 1) - lax.broadcasted_iota(jnp.int32, (tq, w), 0)) + offset
    return jnp.where((rel <= half) & (rel >= -half), 0.0, NEG_INF).astype(F32)


def _fill_band_biases(bias_scr, block, win, half):
    for i, off in enumerate((0, -half, -2 * half)):
        bias_scr[i] = _band_bias(block, win, off, half)


def _band_bias_index(blk, n_blocks):
    return jnp.where(blk == 0, 0, jnp.where(blk == n_blocks - 1, 2, 1))


A_BLOCK = 128
A_WIN = A_BLOCK + 2 * HALF_A


def _attn_a_kernel(q1_ref, k1_ref, v1_ref, q4_ref, k4_ref, v4_ref, q16_ref, k16_ref, v16_ref, o_ref,
                   o1_scr, l1_scr, o4_scr, l4_scr, out_scr, bias_scr):
    seq = q1_ref.shape[2]
    len4, len16 = seq // 4, seq // 16
    _fill_band_biases(bias_scr, A_BLOCK, A_WIN, HALF_A)

    def block(q_ref, k_ref, v_ref, lead, base, blk, n_blocks, o_scr, l_scr):
        q0 = pl.multiple_of(blk * A_BLOCK, A_BLOCK)
        win = pl.ds(pl.multiple_of(jnp.clip(q0 - HALF_A, 0, n_blocks * A_BLOCK - A_WIN), HALF_A), A_WIN)
        bias = bias_scr[_band_bias_index(blk, n_blocks)]
        acc, m, l = _pair_attention(q_ref[lead + (pl.ds(q0, A_BLOCK),)], k_ref[lead + (win,)],
                                    v_ref[lead + (win,)], bias, bias)
        o_scr[pl.ds(base + q0, A_BLOCK), :] = acc / l
        l_scr[pl.ds(base + q0, A_BLOCK), :] = m + jnp.log(l)

    def body1(blk, carry):
        block(q1_ref, k1_ref, v1_ref, (0, 0), 0, blk, seq // A_BLOCK, o1_scr, l1_scr)
        return carry
    lax.fori_loop(0, seq // A_BLOCK, body1, 0, unroll=4)

    def body4(blk, carry):
        for res in range(4):
            block(q4_ref, k4_ref, v4_ref, (0, 0, res), res * len4, blk, len4 // A_BLOCK, o4_scr, l4_scr)
        return carry
    lax.fori_loop(0, len4 // A_BLOCK, body4, 0)

    bias16 = _band_bias(len16, len16, 0, HALF_A)
    for res in range(16):
        acc, m, l = _pair_attention(q16_ref[0, 0, res], k16_ref[0, 0, res], v16_ref[0, 0, res], bias16, bias16)
        o16 = acc / l
        lse16 = m + jnp.log(l)
        o1 = o1_scr[pl.ds(res, len16, stride=16), :]
        lse1 = l1_scr[pl.ds(res, len16, stride=16), :]
        r4 = (res % 4) * len4 + res // 4
        o4 = o4_scr[pl.ds(r4, len16, stride=4), :]
        lse4 = l4_scr[pl.ds(r4, len16, stride=4), :]
        mx = jnp.maximum(jnp.maximum(lse1, lse4), lse16)
        e1 = jnp.exp(lse1 - mx)
        e4 = jnp.exp(lse4 - mx)
        e16 = jnp.exp(lse16 - mx)
        out_scr[pl.ds(res, len16, stride=16), :] = (e1 * o1 + e4 * o4 + e16 * o16) / (e1 + e4 + e16)
    o_ref[0, 0] = out_scr[...].astype(o_ref.dtype)


def _attn_a(nat, s4, s16, batch, seq):
    in_specs, args = [], []
    for arr, shape in ((nat, (seq, LANES)), (s4, (4, seq // 4, LANES)), (s16, (16, seq // 16, LANES))):
        zeros = (0,) * len(shape)
        for goff in (0, PAIRS_A, 2 * PAIRS_A):
            in_specs.append(pl.BlockSpec((1, 1) + shape, lambda b, g, goff=goff, zeros=zeros: (b, goff + g) + zeros))
            args.append(arr)
    return pl.pallas_call(
        _attn_a_kernel,
        grid=(batch, PAIRS_A),
        in_specs=in_specs,
        out_specs=pl.BlockSpec((1, 1, seq, LANES), lambda b, g: (b, g, 0, 0)),
        out_shape=jax.ShapeDtypeStruct((batch, PAIRS_A, seq, LANES), BF16),
        scratch_shapes=[pltpu.VMEM((seq, LANES), F32)] * 5 + [pltpu.VMEM((3, A_BLOCK, A_WIN), F32)],
        compiler_params=_cparams("parallel", "parallel"),
    )(*args)


B_BLOCK = 128
B_WIN = B_BLOCK + 2 * WINDOW_B


def _attn_b_kernel(q_ref, k_ref, v_ref, sink_ref, o_ref, ks, vs, bias_scr):
    p = pl.program_id(1)
    seq = q_ref.shape[2]
    n_blocks = seq // B_BLOCK
    heads_per_kv = N_HEADS_B // N_KV_B
    _fill_band_biases(bias_scr, B_BLOCK, B_WIN, WINDOW_B)
    swap0 = (2 * p) // heads_per_kv != 0
    swap1 = (2 * p + 1) // heads_per_kv != 1

    def place(x_ref, dst):
        x = x_ref[0, 0]
        xs = pltpu.roll(x, HEAD_DIM, 1)
        lo = lax.broadcasted_iota(jnp.int32, x.shape, 1) < HEAD_DIM
        dst[...] = jnp.where(lo, jnp.where(swap0, xs, x), jnp.where(swap1, xs, x))

    place(k_ref, ks)
    place(v_ref, vs)
    sink = sink_ref[0]

    def body(blk, carry):
        q0 = pl.multiple_of(blk * B_BLOCK, B_BLOCK)
        win = pl.ds(pl.multiple_of(jnp.clip(q0 - WINDOW_B, 0, seq - B_WIN), B_BLOCK), B_WIN)
        bias = bias_scr[_band_bias_index(blk, n_blocks)]
        acc, m, l = _pair_attention(q_ref[0, 0, pl.ds(q0, B_BLOCK), :], ks[win], vs[win], bias, bias)
        o_ref[0, 0, pl.ds(q0, B_BLOCK), :] = (acc / (l + jnp.exp(sink - m))).astype(o_ref.dtype)
        return carry

    lax.fori_loop(0, n_blocks, body, 0, unroll=4)


def _attn_b(bc, sink_lanes, batch, seq):
    blk = (1, 1, seq, LANES)
    return pl.pallas_call(
        _attn_b_kernel,
        grid=(batch, PAIRS_B),
        in_specs=[pl.BlockSpec(blk, lambda b, p: (b, G_QB + p, 0, 0)),
                  pl.BlockSpec(blk, lambda b, p: (b, G_KB, 0, 0)),
                  pl.BlockSpec(blk, lambda b, p: (b, G_VB, 0, 0)),
                  pl.BlockSpec((1, 1, LANES), lambda b, p: (p, 0, 0))],
        out_specs=pl.BlockSpec(blk, lambda b, p: (b, p, 0, 0)),
        out_shape=jax.ShapeDtypeStruct((batch, PAIRS_B, seq, LANES), BF16),
        scratch_shapes=[pltpu.VMEM((seq, LANES), BF16)] * 2 + [pltpu.VMEM((3, B_BLOCK, B_WIN), F32)],
        compiler_params=_cparams("parallel", "parallel"),
    )(bc, bc, bc, sink_lanes)


def _attn_c_kernel(q_ref, k_ref, v_ref, tab_ref, o_ref):
    seq = q_ref.shape[2]
    rows = seq // GRID_W
    kr = min(NA_ROWS, rows)
    n_keys = kr * GRID_W

    def body(r, carry):
        rs = jnp.clip(r - kr // 2, 0, rows - kr)
        q0 = pl.multiple_of(r * GRID_W, GRID_W)
        win = pl.ds(pl.multiple_of(rs * GRID_W, GRID_W), n_keys)
        d = r - rs
        acc, m, l = _pair_attention(q_ref[0, 0, pl.ds(q0, GRID_W), :], k_ref[0, 0, win, :], v_ref[0, 0, win, :],
                                    tab_ref[0, 0, d], tab_ref[0, 1, d])
        o_ref[0, 0, pl.ds(q0, GRID_W), :] = (acc / l).astype(o_ref.dtype)
        return carry

    lax.fori_loop(0, rows, body, 0, unroll=4)


def _attn_c(bc, table, batch, seq):
    blk = (1, 1, seq, LANES)
    return pl.pallas_call(
        _attn_c_kernel,
        grid=(PAIRS_C, batch),
        in_specs=[pl.BlockSpec(blk, lambda p, b: (b, G_QC + p, 0, 0)),
                  pl.BlockSpec(blk, lambda p, b: (b, G_KC + p, 0, 0)),
                  pl.BlockSpec(blk, lambda p, b: (b, G_VC + p, 0, 0)),
                  pl.BlockSpec((1,) + table.shape[1:], lambda p, b: (p, 0, 0, 0, 0))],
        out_specs=pl.BlockSpec(blk, lambda p, b: (b, p, 0, 0)),
        out_shape=jax.ShapeDtypeStruct((batch, PAIRS_C, seq, LANES), BF16),
        compiler_params=_cparams("parallel", "parallel"),
    )(bc, bc, bc, table)


def _neighbourhood_table(rpb, seq):
    rows = seq // GRID_W
    kr = min(NA_ROWS, rows)
    c = np.arange(GRID_W)
    coff = np.clip(c[None, :] - c[:, None] + (NA_COLS - 1), 0, 2 * NA_COLS - 2)
    col_start = np.clip(c - NA_COLS // 2, 0, GRID_W - NA_COLS)
    col_mask = (c[None, :] >= col_start[:, None]) & (c[None, :] < col_start[:, None] + NA_COLS)
    per_row = jnp.where(col_mask, rpb.astype(F32)[:, :, coff], NEG_INF)
    h = rpb.shape[0]
    classes = []
    for d in range(kr):
        lo = NA_ROWS - 1 - d
        classes.append(per_row[:, lo:lo + kr].transpose(0, 2, 1, 3).reshape(h, GRID_W, kr * GRID_W))
    return jnp.stack(classes, axis=1).reshape(h // 2, 2, kr, GRID_W, kr * GRID_W)


def _outproj_kernel(oa_ref, ob_ref, oc_ref, gain_ref, w_ref, x_ref, ln_ref, xo_ref, ho_ref, mixed_scr):
    col = 0
    for ref in (oa_ref, ob_ref, oc_ref):
        n = ref.shape[1]
        o = ref[0].astype(F32)
        ss = jnp.sum(jnp.sum(o * o, axis=0), axis=-1, keepdims=True)
        inv = lax.rsqrt(ss / (n * LANES) + EPS)
        for g in range(n):
            sl = slice(col * LANES, (col + 1) * LANES)
            mixed_scr[:, sl] = (o[g] * inv * gain_ref[:, sl]).astype(BF16)
            col += 1
    mixed = mixed_scr[...]
    d = x_ref.shape[1]
    ssq = jnp.zeros((x_ref.shape[0], 1), F32)
    for c in range(d // MXU_N):
        sl = slice(c * MXU_N, (c + 1) * MXU_N)
        xn = x_ref[:, sl] + jnp.dot(mixed, w_ref[:, sl], preferred_element_type=F32)
        xo_ref[:, sl] = xn
        ssq = ssq + jnp.sum(xn * xn, axis=-1, keepdims=True)
    inv = lax.rsqrt(ssq / d + EPS)
    ho_ref[...] = (xo_ref[...] * inv * ln_ref[...]).astype(ho_ref.dtype)


def _outproj(oa, ob, oc, gain, w, x2d, ln, batch, seq, tm=512):
    m, d = x2d.shape
    tpb = seq // tm

    def ospec(o):
        return pl.BlockSpec((1, o.shape[1], tm, LANES), lambda i: (i // tpb, 0, i % tpb, 0))

    row = pl.BlockSpec((tm, d), lambda i: (i, 0))
    vec = pl.BlockSpec((1, d), lambda i: (0, 0))
    return pl.pallas_call(
        _outproj_kernel,
        grid=(m // tm,),
        in_specs=[ospec(oa), ospec(ob), ospec(oc), vec, _resident(w.shape, lambda i: (0, 0)), row, vec],
        out_specs=[row, row],
        out_shape=[jax.ShapeDtypeStruct((m, d), F32), jax.ShapeDtypeStruct((m, d), BF16)],
        scratch_shapes=[pltpu.VMEM((tm, d), BF16)],
        compiler_params=_cparams("parallel"),
    )(oa, ob, oc, gain.reshape(1, d), w, x2d, ln.reshape(1, d))


def _ffn_up_kernel(h_ref, wg_ref, wv_ref, cwg_ref, cwv_ref, cbg_ref, cbv_ref, o_ref):
    h = h_ref[...]
    seq = h.shape[0]
    row = lax.broadcasted_iota(jnp.int32, (seq, MXU_N), 0)
    first, last = row == 0, row == seq - 1

    def conv(u, cw, cb):
        prev = jnp.where(first, 0.0, pltpu.roll(u, 1, 0))
        nxt = jnp.where(last, 0.0, pltpu.roll(u, seq - 1, 0))
        return prev * cw[0:1, :] + u * cw[1:2, :] + nxt * cw[2:3, :] + cb

    for c in range(wg_ref.shape[1] // MXU_N):
        sl = slice(c * MXU_N, (c + 1) * MXU_N)
        g = conv(jnp.dot(h, wg_ref[:, sl], preferred_element_type=F32), cwg_ref[:, sl], cbg_ref[:, sl])
        v = conv(jnp.dot(h, wv_ref[:, sl], preferred_element_type=F32), cwv_ref[:, sl], cbv_ref[:, sl])
        o_ref[:, sl] = (g / (1.0 + jnp.exp(-g)) * v).astype(o_ref.dtype)


def _ffn_up(h, w_up, conv_w, conv_b, batch, seq, tf=512):
    m, d = h.shape
    d_ff = w_up.shape[1] // 2
    nf = d_ff // tf
    cw = conv_w.shape[0]
    conv_b = conv_b.reshape(1, 2 * d_ff)
    return pl.pallas_call(
        _ffn_up_kernel,
        grid=(batch, nf),
        in_specs=[pl.BlockSpec((seq, d), lambda b, j: (b, 0)),
                  pl.BlockSpec((d, tf), lambda b, j: (0, j)),
                  pl.BlockSpec((d, tf), lambda b, j: (0, j + nf)),
                  pl.BlockSpec((cw, tf), lambda b, j: (0, j)),
                  pl.BlockSpec((cw, tf), lambda b, j: (0, j + nf)),
                  pl.BlockSpec((1, tf), lambda b, j: (0, j)),
                  pl.BlockSpec((1, tf), lambda b, j: (0, j + nf))],
        out_specs=pl.BlockSpec((seq, tf), lambda b, j: (b, j)),
        out_shape=jax.ShapeDtypeStruct((m, d_ff), BF16),
        compiler_params=_cparams("parallel", "arbitrary"),
    )(h, w_up, w_up, conv_w, conv_w, conv_b, conv_b)


def _ffn_down_kernel(a_ref, w_ref, x_ref, ln_ref, *out_refs, final):
    xo_ref = out_refs[0]
    a = a_ref[...]
    d = x_ref.shape[1]
    ssq = jnp.zeros((a.shape[0], 1), F32)
    for c in range(d // MXU_N):
        sl = slice(c * MXU_N, (c + 1) * MXU_N)
        xn = x_ref[:, sl] + jnp.dot(a, w_ref[:, sl], preferred_element_type=F32)
        xo_ref[:, sl] = xn
        ssq = ssq + jnp.sum(xn * xn, axis=-1, keepdims=True)
    hn = xo_ref[...] * lax.rsqrt(ssq / d + EPS) * ln_ref[...]
    if final:
        xo_ref[...] = hn
    else:
        out_refs[1][...] = hn.astype(out_refs[1].dtype)


def _ffn_down(act, w, x2d, ln, final, tm=256):
    m, d = x2d.shape
    kdim = act.shape[1]
    row = pl.BlockSpec((tm, d), lambda i: (i, 0))
    if final:
        out_specs, out_shape = row, jax.ShapeDtypeStruct((m, d), F32)
    else:
        out_specs = [row, row]
        out_shape = [jax.ShapeDtypeStruct((m, d), F32), jax.ShapeDtypeStruct((m, d), BF16)]
    return pl.pallas_call(
        functools.partial(_ffn_down_kernel, final=final),
        grid=(m // tm,),
        in_specs=[pl.BlockSpec((tm, kdim), lambda i: (i, 0)), _resident(w.shape, lambda i: (0, 0)),
                  row, pl.BlockSpec((1, d), lambda i: (0, 0))],
        out_specs=out_specs,
        out_shape=out_shape,
        compiler_params=_cparams("parallel"),
    )(act, w, x2d, ln.reshape(1, d))


def _rotary_tables(seq):
    inv_freq = ROPE_THETA ** (-jnp.arange(0, HEAD_DIM, 2, dtype=F32) / HEAD_DIM)
    ang = jnp.arange(seq, dtype=F32)[:, None] * inv_freq[None, :]
    cos, sin = jnp.cos(ang), jnp.sin(ang)
    cos = jnp.tile(cos, (1, 2 * LANES // HEAD_DIM))
    sin = jnp.tile(jnp.concatenate([-sin, sin], axis=-1), (1, LANES // HEAD_DIM))
    return cos, sin


def kernel(x, ln_attn, w_in, sink_b, rpb_c, mix_gain, w_out, ln_ffn, w_up, conv_w, conv_b, w_down, ln_final):
    batch, seq, d = x.shape
    depth = w_in.shape[0]
    m = batch * seq
    x2d = x.reshape(m, d)
    cos, sin = _rotary_tables(seq)
    h = _rmsnorm(x2d, ln_attn[0], BF16)
    for l in range(depth):
        nat, s4, s16, bc = _inproj(h, w_in[l].astype(BF16), cos, sin, batch, seq)
        sink_lanes = jnp.repeat(sink_b[l].astype(F32), HEAD_DIM).reshape(PAIRS_B, 1, LANES)
        oa = _attn_a(nat, s4, s16, batch, seq)
        ob = _attn_b(bc, sink_lanes, batch, seq)
        oc = _attn_c(bc, _neighbourhood_table(rpb_c[l], seq), batch, seq)
        x2d, h = _outproj(oa, ob, oc, mix_gain[l], w_out[l].astype(BF16), x2d, ln_ffn[l], batch, seq)
        act = _ffn_up(h, w_up[l].astype(BF16), conv_w[l], conv_b[l], batch, seq)
        if l + 1 < depth:
            x2d, h = _ffn_down(act, w_down[l].astype(BF16), x2d, ln_attn[l + 1], final=False)
        else:
            out = _ffn_down(act, w_down[l].astype(BF16), x2d, ln_final, final=True)
    return out.reshape(batch, seq, d)
```

```python
import functools

import jax
import jax.numpy as jnp
import numpy as np
from jax import lax
from jax.experimental import pallas as pl
from jax.experimental.pallas import tpu as pltpu

F32 = jnp.float32
BF16 = jnp.bfloat16

LANES = 128
MXU_N = 256
HEAD_DIM = 64
N_HEADS_A, N_HEADS_B, N_KV_B, N_HEADS_C = 12, 10, 2, 10
PAIRS_A, PAIRS_B, PAIRS_C = N_HEADS_A // 2, N_HEADS_B // 2, N_HEADS_C // 2
HALF_A = 64
WINDOW_B = 128
GRID_W, NA_ROWS, NA_COLS = 64, 8, 16
ROPE_THETA = 10000.0
EPS = 1e-6
NEG_INF = -1e30
LOG2E = 1.4426950408889634
Q_SCALE = HEAD_DIM ** -0.5 * LOG2E

GROUPS_A = 3 * PAIRS_A
G_QB, G_KB, G_VB = 0, PAIRS_B, PAIRS_B + 1
G_QC, G_KC, G_VC = PAIRS_B + 2, PAIRS_B + 2 + PAIRS_C, PAIRS_B + 2 + 2 * PAIRS_C
GROUPS_BC = G_VC + PAIRS_C

VMEM_LIMIT = 56 * 1024 * 1024


def _cparams(*sem):
    return pltpu.CompilerParams(dimension_semantics=sem, vmem_limit_bytes=VMEM_LIMIT)


def _resident(shape, index_map):
    return pl.BlockSpec(shape, index_map, pipeline_mode=pl.Buffered(1))


def _rmsnorm_kernel(x_ref, g_ref, o_ref):
    x = x_ref[...]
    ms = jnp.mean(x * x, axis=-1, keepdims=True)
    o_ref[...] = (x * lax.rsqrt(ms + EPS) * g_ref[...]).astype(o_ref.dtype)


def _rmsnorm(x2d, gain, out_dtype, tm=512):
    m, d = x2d.shape
    return pl.pallas_call(
        _rmsnorm_kernel,
        grid=(m // tm,),
        in_specs=[pl.BlockSpec((tm, d), lambda i: (i, 0)), pl.BlockSpec((1, d), lambda i: (0, 0))],
        out_specs=pl.BlockSpec((tm, d), lambda i: (i, 0)),
        out_shape=jax.ShapeDtypeStruct((m, d), out_dtype),
        compiler_params=_cparams("parallel"),
    )(x2d, gain.reshape(1, d))


N_STAGE = 4


def _inproj_kernel(h_ref, w_ref, cos_ref, sin_ref, nat_ref, s4_ref, s16_ref, bc_ref, stage_scr):
    tm = h_ref.shape[0]
    h = h_ref[...]
    lane = lax.broadcasted_iota(jnp.int32, (tm, LANES), 1)
    first_half = (lane % HEAD_DIM) < (HEAD_DIM // 2)
    cos, sin = cos_ref[...], sin_ref[...]

    def rotary(xg, scale):
        rot = jnp.where(first_half, pltpu.roll(xg, LANES - 32, 1), pltpu.roll(xg, 32, 1))
        y = xg * cos + rot * sin
        return y * scale if scale != 1.0 else y

    n_groups = GROUPS_A + GROUPS_BC
    for c in range(n_groups * LANES // MXU_N):
        acc = jnp.dot(h, w_ref[:, c * MXU_N:(c + 1) * MXU_N], preferred_element_type=F32)
        for gi in range(MXU_N // LANES):
            g = c * (MXU_N // LANES) + gi
            xg = acc[:, gi * LANES:(gi + 1) * LANES]
            if g < GROUPS_A:
                if g < PAIRS_A:
                    y = rotary(xg, Q_SCALE)
                elif g < 2 * PAIRS_A:
                    y = rotary(xg, 1.0)
                else:
                    y = xg
                nat_ref[0, g] = y.astype(nat_ref.dtype)
                base = (g % N_STAGE) * tm
                stage_scr[pl.ds(base, tm), :] = y
                for r in range(4):
                    s4_ref[0, g, r] = stage_scr[pl.ds(base + r, tm // 4, stride=4), :].astype(s4_ref.dtype)
                for r in range(16):
                    s16_ref[0, g, r] = stage_scr[pl.ds(base + r, tm // 16, stride=16), :].astype(s16_ref.dtype)
            else:
                gb = g - GROUPS_A
                if gb < G_KB:
                    y = rotary(xg, Q_SCALE)
                elif gb < G_VB:
                    y = rotary(xg, 1.0)
                elif G_QC <= gb < G_KC:
                    y = xg * Q_SCALE
                else:
                    y = xg
                bc_ref[0, gb] = y.astype(bc_ref.dtype)


def _inproj(h, w, layer, cos, sin, batch, seq, tm=512):
    m, d = h.shape
    n = w.shape[2]
    tpb = seq // tm
    return pl.pallas_call(
        _inproj_kernel,
        grid=(m // tm,),
        in_specs=[
            pl.BlockSpec((tm, d), lambda i: (i, 0)),
            _resident((None, d, n), lambda i: (layer, 0, 0)),
            pl.BlockSpec((tm, LANES), lambda i: (i % tpb, 0)),
            pl.BlockSpec((tm, LANES), lambda i: (i % tpb, 0)),
        ],
        out_specs=[
            pl.BlockSpec((1, GROUPS_A, tm, LANES), lambda i: (i // tpb, 0, i % tpb, 0)),
            pl.BlockSpec((1, GROUPS_A, 4, tm // 4, LANES), lambda i: (i // tpb, 0, 0, i % tpb, 0)),
            pl.BlockSpec((1, GROUPS_A, 16, tm // 16, LANES), lambda i: (i // tpb, 0, 0, i % tpb, 0)),
            pl.BlockSpec((1, GROUPS_BC, tm, LANES), lambda i: (i // tpb, 0, i % tpb, 0)),
        ],
        out_shape=[
            jax.ShapeDtypeStruct((batch, GROUPS_A, seq, LANES), BF16),
            jax.ShapeDtypeStruct((batch, GROUPS_A, 4, seq // 4, LANES), BF16),
            jax.ShapeDtypeStruct((batch, GROUPS_A, 16, seq // 16, LANES), BF16),
            jax.ShapeDtypeStruct((batch, GROUPS_BC, seq, LANES), BF16),
        ],
        scratch_shapes=[pltpu.VMEM((N_STAGE * tm, LANES), F32)],
        compiler_params=_cparams("parallel"),
    )(h, w, cos, sin)


def _pair_attention(q, k, v, bias0, bias1):
    tq = q.shape[0]
    lo = lax.broadcasted_iota(jnp.int32, q.shape, 1) < HEAD_DIM
    zero = jnp.zeros_like(q)
    q2 = jnp.concatenate([jnp.where(lo, q, zero), jnp.where(lo, zero, q)], axis=0)
    s = lax.dot_general(q2, k, (((1,), (1,)), ((), ())), preferred_element_type=F32)
    s = jnp.concatenate([s[:tq] + bias0, s[tq:] + bias1], axis=0)
    m = jnp.max(s, axis=-1, keepdims=True)
    p = jnp.exp2(s - m).astype(BF16)
    r = jnp.dot(p, jnp.concatenate([v, jnp.ones_like(v)], axis=1), preferred_element_type=F32)
    acc = jnp.where(lo, r[:tq, :LANES], r[tq:, :LANES])
    l = jnp.where(lo, r[:tq, LANES:], r[tq:, LANES:])
    return acc, jnp.where(lo, m[:tq], m[tq:]), l


def _band_bias(tq, w, offset, half):
    rel = (lax.broadcasted_iota(jnp.int32, (tq, w), 1) - lax.broadcasted_iota(jnp.int32, (tq, w), 0)) + offset
    return jnp.where((rel <= half) & (rel >= -half), 0.0, NEG_INF).astype(F32)


def _fill_band_biases(bias_scr, block, win, half):
    for i, off in enumerate((0, -half, -2 * half)):
        bias_scr[i] = _band_bias(block, win, off, half)


def _band_bias_index(blk, n_blocks):
    return jnp.where(blk == 0, 0, jnp.where(blk == n_blocks - 1, 2, 1))


A_BLOCK = 128
A_WIN = A_BLOCK + 2 * HALF_A


def _attn_a_kernel(q1_ref, k1_ref, v1_ref, q4_ref, k4_ref, v4_ref, q16_ref, k16_ref, v16_ref, o_ref,
                   o1_scr, l1_scr, o4_scr, l4_scr, out_scr, bias_scr):
    seq = q1_ref.shape[2]
    len4, len16 = seq // 4, seq // 16
    _fill_band_biases(bias_scr, A_BLOCK, A_WIN, HALF_A)

    def block(q_ref, k_ref, v_ref, lead, base, blk, n_blocks, o_scr, l_scr):
        q0 = pl.multiple_of(blk * A_BLOCK, A_BLOCK)
        win = pl.ds(pl.multiple_of(jnp.clip(q0 - HALF_A, 0, n_blocks * A_BLOCK - A_WIN), HALF_A), A_WIN)
        bias = bias_scr[_band_bias_index(blk, n_blocks)]
        acc, m, l = _pair_attention(q_ref[lead + (pl.ds(q0, A_BLOCK),)], k_ref[lead + (win,)],
                                    v_ref[lead + (win,)], bias, bias)
        o_scr[pl.ds(base + q0, A_BLOCK), :] = acc / l
        l_scr[pl.ds(base + q0, A_BLOCK), :] = m + jnp.log2(l)

    def body1(blk, carry):
        block(q1_ref, k1_ref, v1_ref, (0, 0), 0, blk, seq // A_BLOCK, o1_scr, l1_scr)
        return carry
    lax.fori_loop(0, seq // A_BLOCK, body1, 0, unroll=4)

    def body4(blk, carry):
        for res in range(4):
            block(q4_ref, k4_ref, v4_ref, (0, 0, res), res * len4, blk, len4 // A_BLOCK, o4_scr, l4_scr)
        return carry
    lax.fori_loop(0, len4 // A_BLOCK, body4, 0)

    bias16 = _band_bias(len16, len16, 0, HALF_A)
    for res in range(16):
        acc, m, l = _pair_attention(q16_ref[0, 0, res], k16_ref[0, 0, res], v16_ref[0, 0, res], bias16, bias16)
        o16 = acc / l
        lse16 = m + jnp.log2(l)
        o1 = o1_scr[pl.ds(res, len16, stride=16), :]
        lse1 = l1_scr[pl.ds(res, len16, stride=16), :]
        r4 = (res % 4) * len4 + res // 4
        o4 = o4_scr[pl.ds(r4, len16, stride=4), :]
        lse4 = l4_scr[pl.ds(r4, len16, stride=4), :]
        mx = jnp.maximum(jnp.maximum(lse1, lse4), lse16)
        e1 = jnp.exp2(lse1 - mx)
        e4 = jnp.exp2(lse4 - mx)
        e16 = jnp.exp2(lse16 - mx)
        out_scr[pl.ds(res, len16, stride=16), :] = (e1 * o1 + e4 * o4 + e16 * o16) / (e1 + e4 + e16)
    o_ref[0, 0] = out_scr[...].astype(o_ref.dtype)


def _attn_a(nat, s4, s16, batch, seq):
    in_specs, args = [], []
    for arr, shape in ((nat, (seq, LANES)), (s4, (4, seq // 4, LANES)), (s16, (16, seq // 16, LANES))):
        zeros = (0,) * len(shape)
        for goff in (0, PAIRS_A, 2 * PAIRS_A):
            in_specs.append(pl.BlockSpec((1, 1) + shape, lambda b, g, goff=goff, zeros=zeros: (b, goff + g) + zeros))
            args.append(arr)
    return pl.pallas_call(
        _attn_a_kernel,
        grid=(batch, PAIRS_A),
        in_specs=in_specs,
        out_specs=pl.BlockSpec((1, 1, seq, LANES), lambda b, g: (b, g, 0, 0)),
        out_shape=jax.ShapeDtypeStruct((batch, PAIRS_A, seq, LANES), BF16),
        scratch_shapes=[pltpu.VMEM((seq, LANES), F32)] * 5 + [pltpu.VMEM((3, A_BLOCK, A_WIN), F32)],
        compiler_params=_cparams("parallel", "parallel"),
    )(*args)


B_BLOCK = 128
B_WIN = B_BLOCK + 2 * WINDOW_B


def _attn_b_kernel(q_ref, k_ref, v_ref, sink_ref, o_ref, ks, vs, bias_scr):
    p = pl.program_id(1)
    seq = q_ref.shape[2]
    n_blocks = seq // B_BLOCK
    heads_per_kv = N_HEADS_B // N_KV_B
    _fill_band_biases(bias_scr, B_BLOCK, B_WIN, WINDOW_B)
    swap0 = (2 * p) // heads_per_kv != 0
    swap1 = (2 * p + 1) // heads_per_kv != 1

    def place(x_ref, dst):
        x = x_ref[0, 0]
        xs = pltpu.roll(x, HEAD_DIM, 1)
        lo = lax.broadcasted_iota(jnp.int32, x.shape, 1) < HEAD_DIM
        dst[...] = jnp.where(lo, jnp.where(swap0, xs, x), jnp.where(swap1, xs, x))

    place(k_ref, ks)
    place(v_ref, vs)
    sink = sink_ref[0]

    def body(blk, carry):
        q0 = pl.multiple_of(blk * B_BLOCK, B_BLOCK)
        win = pl.ds(pl.multiple_of(jnp.clip(q0 - WINDOW_B, 0, seq - B_WIN), B_BLOCK), B_WIN)
        bias = bias_scr[_band_bias_index(blk, n_blocks)]
        acc, m, l = _pair_attention(q_ref[0, 0, pl.ds(q0, B_BLOCK), :], ks[win], vs[win], bias, bias)
        o_ref[0, 0, pl.ds(q0, B_BLOCK), :] = (acc / (l + jnp.exp2(sink - m))).astype(o_ref.dtype)
        return carry

    lax.fori_loop(0, n_blocks, body, 0, unroll=4)


def _attn_b(bc, sink_lanes, batch, seq):
    blk = (1, 1, seq, LANES)
    return pl.pallas_call(
        _attn_b_kernel,
        grid=(batch, PAIRS_B),
        in_specs=[pl.BlockSpec(blk, lambda b, p: (b, G_QB + p, 0, 0)),
                  pl.BlockSpec(blk, lambda b, p: (b, G_KB, 0, 0)),
                  pl.BlockSpec(blk, lambda b, p: (b, G_VB, 0, 0)),
                  pl.BlockSpec((1, 1, LANES), lambda b, p: (p, 0, 0))],
        out_specs=pl.BlockSpec(blk, lambda b, p: (b, p, 0, 0)),
        out_shape=jax.ShapeDtypeStruct((batch, PAIRS_B, seq, LANES), BF16),
        scratch_shapes=[pltpu.VMEM((seq, LANES), BF16)] * 2 + [pltpu.VMEM((3, B_BLOCK, B_WIN), F32)],
        compiler_params=_cparams("parallel", "parallel"),
    )(bc, bc, bc, sink_lanes)


def _attn_c_kernel(q_ref, k_ref, v_ref, tab_ref, o_ref):
    seq = q_ref.shape[2]
    rows = seq // GRID_W
    kr = min(NA_ROWS, rows)
    n_keys = kr * GRID_W

    def body(r, carry):
        rs = jnp.clip(r - kr // 2, 0, rows - kr)
        q0 = pl.multiple_of(r * GRID_W, GRID_W)
        win = pl.ds(pl.multiple_of(rs * GRID_W, GRID_W), n_keys)
        d = r - rs
        acc, m, l = _pair_attention(q_ref[0, 0, pl.ds(q0, GRID_W), :], k_ref[0, 0, win, :], v_ref[0, 0, win, :],
                                    tab_ref[0, 0, d], tab_ref[0, 1, d])
        o_ref[0, 0, pl.ds(q0, GRID_W), :] = (acc / l).astype(o_ref.dtype)
        return carry

    lax.fori_loop(0, rows, body, 0, unroll=4)


def _attn_c(bc, table, batch, seq):
    blk = (1, 1, seq, LANES)
    return pl.pallas_call(
        _attn_c_kernel,
        grid=(PAIRS_C, batch),
        in_specs=[pl.BlockSpec(blk, lambda p, b: (b, G_QC + p, 0, 0)),
                  pl.BlockSpec(blk, lambda p, b: (b, G_KC + p, 0, 0)),
                  pl.BlockSpec(blk, lambda p, b: (b, G_VC + p, 0, 0)),
                  pl.BlockSpec((1,) + table.shape[1:], lambda p, b: (p, 0, 0, 0, 0))],
        out_specs=pl.BlockSpec(blk, lambda p, b: (b, p, 0, 0)),
        out_shape=jax.ShapeDtypeStruct((batch, PAIRS_C, seq, LANES), BF16),
        compiler_params=_cparams("parallel", "parallel"),
    )(bc, bc, bc, table)


def _neighbourhood_table(rpb, seq):
    rows = seq // GRID_W
    kr = min(NA_ROWS, rows)
    c = np.arange(GRID_W)
    coff = np.clip(c[None, :] - c[:, None] + (NA_COLS - 1), 0, 2 * NA_COLS - 2)
    col_start = np.clip(c - NA_COLS // 2, 0, GRID_W - NA_COLS)
    col_mask = (c[None, :] >= col_start[:, None]) & (c[None, :] < col_start[:, None] + NA_COLS)
    per_row = jnp.where(col_mask, rpb.astype(F32)[:, :, coff], NEG_INF)
    h = rpb.shape[0]
    classes = []
    for d in range(kr):
        lo = NA_ROWS - 1 - d
        classes.append(per_row[:, lo:lo + kr].transpose(0, 2, 1, 3).reshape(h, GRID_W, kr * GRID_W))
    return jnp.stack(classes, axis=1).reshape(h // 2, 2, kr, GRID_W, kr * GRID_W)


def _outproj_kernel(oa_ref, ob_ref, oc_ref, gain_ref, w_ref, x_ref, ln_ref, xo_ref, ho_ref, mixed_scr):
    col = 0
    for ref in (oa_ref, ob_ref, oc_ref):
        n = ref.shape[1]
        o = ref[0].astype(F32)
        ss = jnp.sum(jnp.sum(o * o, axis=0), axis=-1, keepdims=True)
        inv = lax.rsqrt(ss / (n * LANES) + EPS)
        for g in range(n):
            sl = slice(col * LANES, (col + 1) * LANES)
            mixed_scr[:, sl] = (o[g] * inv * gain_ref[:, sl]).astype(BF16)
            col += 1
    mixed = mixed_scr[...]
    d = x_ref.shape[1]
    ssq = jnp.zeros((x_ref.shape[0], 1), F32)
    for c in range(d // MXU_N):
        sl = slice(c * MXU_N, (c + 1) * MXU_N)
        xn = x_ref[:, sl] + jnp.dot(mixed, w_ref[:, sl], preferred_element_type=F32)
        xo_ref[:, sl] = xn
        ssq = ssq + jnp.sum(xn * xn, axis=-1, keepdims=True)
    inv = lax.rsqrt(ssq / d + EPS)
    ho_ref[...] = (xo_ref[...] * inv * ln_ref[...]).astype(ho_ref.dtype)


def _outproj(oa, ob, oc, gain, w, layer, x2d, ln, batch, seq, tm=512):
    m, d = x2d.shape
    tpb = seq // tm

    def ospec(o):
        return pl.BlockSpec((1, o.shape[1], tm, LANES), lambda i: (i // tpb, 0, i % tpb, 0))

    row = pl.BlockSpec((tm, d), lambda i: (i, 0))
    vec = pl.BlockSpec((1, d), lambda i: (0, 0))
    return pl.pallas_call(
        _outproj_kernel,
        grid=(m // tm,),
        in_specs=[ospec(oa), ospec(ob), ospec(oc), vec,
                  _resident((None,) + w.shape[1:], lambda i: (layer, 0, 0)), row, vec],
        out_specs=[row, row],
        out_shape=[jax.ShapeDtypeStruct((m, d), F32), jax.ShapeDtypeStruct((m, d), BF16)],
        scratch_shapes=[pltpu.VMEM((tm, d), BF16)],
        compiler_params=_cparams("parallel"),
    )(oa, ob, oc, gain.reshape(1, d), w, x2d, ln.reshape(1, d))


def _ffn_up_kernel(h_ref, wg_ref, wv_ref, cwg_ref, cwv_ref, cbg_ref, cbv_ref, o_ref):
    h = h_ref[...]
    seq = h.shape[0]
    row = lax.broadcasted_iota(jnp.int32, (seq, MXU_N), 0)
    first, last = row == 0, row == seq - 1

    def conv(u, cw, cb):
        prev = jnp.where(first, 0.0, pltpu.roll(u, 1, 0))
        nxt = jnp.where(last, 0.0, pltpu.roll(u, seq - 1, 0))
        return prev * cw[0:1, :] + u * cw[1:2, :] + nxt * cw[2:3, :] + cb

    for c in range(wg_ref.shape[1] // MXU_N):
        sl = slice(c * MXU_N, (c + 1) * MXU_N)
        g = conv(jnp.dot(h, wg_ref[:, sl], preferred_element_type=F32), cwg_ref[:, sl], cbg_ref[:, sl])
        v = conv(jnp.dot(h, wv_ref[:, sl], preferred_element_type=F32), cwv_ref[:, sl], cbv_ref[:, sl])
        o_ref[:, sl] = (g / (1.0 + jnp.exp(-g)) * v).astype(o_ref.dtype)


def _ffn_up(h, w_up, layer, conv_w, conv_b, batch, seq, tf=512):
    m, d = h.shape
    d_ff = w_up.shape[2] // 2
    nf = d_ff // tf
    cw = conv_w.shape[0]
    conv_b = conv_b.reshape(1, 2 * d_ff)
    return pl.pallas_call(
        _ffn_up_kernel,
        grid=(batch, nf),
        in_specs=[pl.BlockSpec((seq, d), lambda b, j: (b, 0)),
                  pl.BlockSpec((None, d, tf), lambda b, j: (layer, 0, j)),
                  pl.BlockSpec((None, d, tf), lambda b, j: (layer, 0, j + nf)),
                  pl.BlockSpec((cw, tf), lambda b, j: (0, j)),
                  pl.BlockSpec((cw, tf), lambda b, j: (0, j + nf)),
                  pl.BlockSpec((1, tf), lambda b, j: (0, j)),
                  pl.BlockSpec((1, tf), lambda b, j: (0, j + nf))],
        out_specs=pl.BlockSpec((seq, tf), lambda b, j: (b, j)),
        out_shape=jax.ShapeDtypeStruct((m, d_ff), BF16),
        compiler_params=_cparams("parallel", "arbitrary"),
    )(h, w_up, w_up, conv_w, conv_w, conv_b, conv_b)


def _ffn_down_kernel(a_ref, w_ref, x_ref, ln_ref, *out_refs, final):
    xo_ref = out_refs[0]
    a = a_ref[...]
    d = x_ref.shape[1]
    ssq = jnp.zeros((a.shape[0], 1), F32)
    for c in range(d // MXU_N):
        sl = slice(c * MXU_N, (c + 1) * MXU_N)
        xn = x_ref[:, sl] + jnp.dot(a, w_ref[:, sl], preferred_element_type=F32)
        xo_ref[:, sl] = xn
        ssq = ssq + jnp.sum(xn * xn, axis=-1, keepdims=True)
    hn = xo_ref[...] * lax.rsqrt(ssq / d + EPS) * ln_ref[...]
    if final:
        xo_ref[...] = hn
    else:
        out_refs[1][...] = hn.astype(out_refs[1].dtype)


def _ffn_down(act, w, layer, x2d, ln, final, tm=256):
    m, d = x2d.shape
    kdim = act.shape[1]
    row = pl.BlockSpec((tm, d), lambda i: (i, 0))
    if final:
        out_specs, out_shape = row, jax.ShapeDtypeStruct((m, d), F32)
    else:
        out_specs = [row, row]
        out_shape = [jax.ShapeDtypeStruct((m, d), F32), jax.ShapeDtypeStruct((m, d), BF16)]
    return pl.pallas_call(
        functools.partial(_ffn_down_kernel, final=final),
        grid=(m // tm,),
        in_specs=[pl.BlockSpec((tm, kdim), lambda i: (i, 0)),
                  _resident((None,) + w.shape[1:], lambda i: (layer, 0, 0)),
                  row, pl.BlockSpec((1, d), lambda i: (0, 0))],
        out_specs=out_specs,
        out_shape=out_shape,
        compiler_params=_cparams("parallel"),
    )(act, w, x2d, ln.reshape(1, d))


def _rotary_tables(seq):
    inv_freq = ROPE_THETA ** (-jnp.arange(0, HEAD_DIM, 2, dtype=F32) / HEAD_DIM)
    ang = jnp.arange(seq, dtype=F32)[:, None] * inv_freq[None, :]
    cos, sin = jnp.cos(ang), jnp.sin(ang)
    cos = jnp.tile(cos, (1, 2 * LANES // HEAD_DIM))
    sin = jnp.tile(jnp.concatenate([-sin, sin], axis=-1), (1, LANES // HEAD_DIM))
    return cos, sin


def kernel(x, ln_attn, w_in, sink_b, rpb_c, mix_gain, w_out, ln_ffn, w_up, conv_w, conv_b, w_down, ln_final):
    batch, seq, d = x.shape
    depth = w_in.shape[0]
    m = batch * seq
    x2d = x.reshape(m, d)
    cos, sin = _rotary_tables(seq)
    w_in, w_out, w_up, w_down = (w.astype(BF16) for w in (w_in, w_out, w_up, w_down))
    h = _rmsnorm(x2d, ln_attn[0], BF16)
    for l in range(depth):
        nat, s4, s16, bc = _inproj(h, w_in, l, cos, sin, batch, seq)
        sink_lanes = jnp.repeat(sink_b[l].astype(F32) * LOG2E, HEAD_DIM).reshape(PAIRS_B, 1, LANES)
        oa = _attn_a(nat, s4, s16, batch, seq)
        ob = _attn_b(bc, sink_lanes, batch, seq)
        oc = _attn_c(bc, _neighbourhood_table(rpb_c[l] * LOG2E, seq), batch, seq)
        x2d, h = _outproj(oa, ob, oc, mix_gain[l], w_out, l, x2d, ln_ffn[l], batch, seq)
        act = _ffn_up(h, w_up, l, conv_w[l], conv_b[l], batch, seq)
        if l + 1 < depth:
            x2d, h = _ffn_down(act, w_down, l, x2d, ln_attn[l + 1], final=False)
        else:
            out = _ffn_down(act, w_down, l, x2d, ln_final, final=True)
    return out.reshape(batch, seq, d)
```
